```python
import jax, jax.numpy as jnp
from jax import lax
import numpy as np

D_MODEL = 1024
BATCH = 8
SEQ = 2048
DEPTH = 4
DEC_BATCH = 16
DEC_SEQ = 4096
PAST_LEN = 128

MIX_WIDTH = D_MODEL
ATTN_WIDTH = MIX_WIDTH // 2
REC_WIDTH = MIX_WIDTH - ATTN_WIDTH
ATTN_HEAD_DIM = 64
ATTN_HEADS = ATTN_WIDTH // ATTN_HEAD_DIM
ROT_DIM = ATTN_HEAD_DIM // 4
ROPE_THETA = 500000.0
DILATED_BRANCHES = ((128, 1), (512, 4), (2048, 16))
REC_HEAD_DIM = 128
REC_HEADS = REC_WIDTH // REC_HEAD_DIM
REC_CHUNK = 64
MEM_TOKENS = 256
CROSS_HEADS = 4
CROSS_HEAD_DIM = 128
CROSS_WIDTH = CROSS_HEADS * CROSS_HEAD_DIM
D_FF = 3584
N_EXPERTS = 8
TOP_K = 2
N_DENSE = (DEPTH + 1) // 2
N_MOE = DEPTH // 2
EPS = 1e-6
IN_WIDTH = 3 * ATTN_WIDTH + 5 * REC_WIDTH
IN_OFFSETS = (ATTN_WIDTH, 2 * ATTN_WIDTH, 3 * ATTN_WIDTH,
              3 * ATTN_WIDTH + REC_WIDTH, 3 * ATTN_WIDTH + 2 * REC_WIDTH,
              3 * ATTN_WIDTH + 3 * REC_WIDTH, 3 * ATTN_WIDTH + 4 * REC_WIDTH)

kernel_name = "hymba_dilated_hgrn2_encoder"


def rmsnorm(x, g):
    xf = x.astype(jnp.float32)
    y = xf * lax.rsqrt(jnp.mean(xf * xf, axis=-1, keepdims=True) + EPS)
    return (y * g.astype(jnp.float32)).astype(x.dtype)


def rope_tables(S):
    inv = ROPE_THETA ** (-jnp.arange(0, ROT_DIM, 2, dtype=jnp.float32) / ROT_DIM)
    ang = jnp.arange(S, dtype=jnp.float32)[:, None] * inv[None, :]
    return jnp.cos(ang)[:, None, :], jnp.sin(ang)[:, None, :]


def apply_partial_rope(x, cos, sin):
    half = ROT_DIM // 2
    xf = x.astype(jnp.float32)
    x1, x2 = xf[..., :half], xf[..., half:ROT_DIM]
    out = jnp.concatenate([x1 * cos - x2 * sin, x2 * cos + x1 * sin, xf[..., ROT_DIM:]], axis=-1)
    return out.astype(x.dtype)


def banded_attention(q, k, v, radius):
    N, L, H, hd = q.shape
    R = radius
    nb = -(-L // R)
    pad = nb * R - L
    qb = jnp.pad(q, ((0, 0), (0, pad), (0, 0), (0, 0))).reshape(N, nb, R, H, hd)

    def neighbour_blocks(t):
        tp = jnp.pad(t, ((0, 0), (R, pad + R), (0, 0), (0, 0))).reshape(N, nb + 2, R, H, hd)
        return jnp.concatenate([tp[:, :-2], tp[:, 1:-1], tp[:, 2:]], axis=2)

    kb = neighbour_blocks(k)
    vb = neighbour_blocks(v)
    qi = jnp.arange(R)[:, None]
    kj = jnp.arange(3 * R)[None, :]
    kpos = jnp.arange(nb)[:, None, None] * R - R + kj[None]
    mask = (jnp.abs(kj - R - qi) <= R)[None] & (kpos >= 0) & (kpos < L)
    s = jnp.einsum('ncihd,ncjhd->nchij', qb, kb).astype(jnp.float32) * (hd ** -0.5)
    s = jnp.where(mask[None, :, None], s, -jnp.inf)
    m = jnp.max(s, axis=-1, keepdims=True)
    p = jnp.exp(s - m)
    l = jnp.sum(p, axis=-1, keepdims=True)
    o = jnp.einsum('nchij,ncjhd->ncihd', p / l, vb.astype(jnp.float32))
    lse = (m + jnp.log(l))[..., 0]
    o = o.reshape(N, nb * R, H, hd)[:, :L]
    lse = lse.transpose(0, 1, 3, 2).reshape(N, nb * R, H)[:, :L]
    return o, lse


def dilated_branch(q, k, v, window, dilation):
    B, S, H, hd = q.shape
    L = S // dilation

    def strided(t):
        return t.reshape(B, L, dilation, H, hd).transpose(0, 2, 1, 3, 4).reshape(B * dilation, L, H, hd)

    o, lse = banded_attention(strided(q), strided(k), strided(v), window // (2 * dilation))
    o = o.reshape(B, dilation, L, H, hd).transpose(0, 2, 1, 3, 4).reshape(B, S, H, hd)
    lse = lse.reshape(B, dilation, L, H).transpose(0, 2, 1, 3).reshape(B, S, H)
    return o, lse


def dilated_attention(q, k, v):
    outs, lses = zip(*[dilated_branch(q, k, v, w, d) for (w, d) in DILATED_BRANCHES])
    wts = jax.nn.softmax(jnp.stack(lses, axis=0), axis=0)
    return jnp.sum(wts[..., None] * jnp.stack(outs, axis=0), axis=0)


def hgrn2_scan(q, k, v, log_f):
    B, S, H, dk = q.shape
    dv = v.shape[-1]
    C = REC_CHUNK
    nc = S // C

    def chunks(t):
        return t.astype(jnp.float32).reshape(B, nc, C, H, t.shape[-1]).transpose(1, 0, 3, 2, 4)

    tri = jnp.tril(jnp.ones((C, C), dtype=bool))[None, None, :, :, None]

    def step(state, inp):
        qc, kc, vc, gc = inp
        b = jnp.cumsum(gc, axis=2)
        b_end = b[:, :, -1:, :]
        o_inter = jnp.einsum('bhtk,bhkv->bhtv', qc * jnp.exp(b), state)
        decay = jnp.exp(jnp.where(tri, b[:, :, :, None, :] - b[:, :, None, :, :], -jnp.inf))
        scores = jnp.einsum('bhtk,bhsk,bhtsk->bhts', qc, kc, decay)
        o_intra = jnp.einsum('bhts,bhsv->bhtv', scores, vc)
        state = jnp.exp(b_end[:, :, 0, :])[..., None] * state + jnp.einsum(
            'bhsk,bhsv->bhkv', kc * jnp.exp(b_end - b), vc)
        return state, o_inter + o_intra

    state0 = jnp.zeros((B, H, dk, dv), jnp.float32)
    _, o = lax.scan(step, state0, (chunks(q), chunks(k), chunks(v), chunks(log_f)))
    return o.transpose(1, 0, 3, 2, 4).reshape(B, S, H, dv)


def hgrn2_bidirectional(q, i, z_fwd, z_bwd, lb_f, lb_b):
    def gates(z, lb):
        zf = z.astype(jnp.float32)
        lb = lb.reshape(REC_HEADS, REC_HEAD_DIM)
        log_f = jnp.log(lb + (1.0 - lb) * jax.nn.sigmoid(zf))
        k = (1.0 - lb) * jax.nn.sigmoid(-zf)
        return k, log_f

    k_f, g_f = gates(z_fwd, lb_f)
    k_b, g_b = gates(z_bwd, lb_b)
    flip = lambda t: jnp.flip(t, axis=1)
    fwd = hgrn2_scan(q, k_f, i, g_f)
    bwd = flip(hgrn2_scan(flip(q), flip(k_b), flip(i), flip(g_b)))
    return fwd + bwd


def parallel_mixer(h, w_in, attn_g, rec_g, lb_f, lb_b, w_out, cos, sin):
    B, S, _ = h.shape
    qa, ka, va, qr, ir, zf, zb, gr = jnp.split(h @ w_in, IN_OFFSETS, axis=-1)
    a_heads = lambda t: t.reshape(B, S, ATTN_HEADS, ATTN_HEAD_DIM)
    r_heads = lambda t: t.reshape(B, S, REC_HEADS, REC_HEAD_DIM)
    att = dilated_attention(apply_partial_rope(a_heads(qa), cos, sin),
                            apply_partial_rope(a_heads(ka), cos, sin), a_heads(va))
    att = rmsnorm(att, attn_g.reshape(ATTN_HEADS, ATTN_HEAD_DIM))
    rec = hgrn2_bidirectional(r_heads(qr), r_heads(ir), r_heads(zf), r_heads(zb), lb_f, lb_b)
    rec = rmsnorm(rec, rec_g.reshape(REC_HEADS, REC_HEAD_DIM)) * r_heads(jax.nn.silu(gr.astype(jnp.float32)))
    mixed = jnp.concatenate([att.reshape(B, S, ATTN_WIDTH), rec.reshape(B, S, REC_WIDTH)], axis=-1)
    return (mixed.astype(h.dtype) @ w_out).astype(h.dtype)


def memory_cross_attention(h, mem_n, w_q, w_kv, w_o):
    B, S, _ = h.shape
    M = mem_n.shape[1]
    q = (h @ w_q).reshape(B, S, CROSS_HEADS, CROSS_HEAD_DIM)
    k, v = jnp.split(mem_n @ w_kv, 2, axis=-1)
    k = k.reshape(B, M, CROSS_HEADS, CROSS_HEAD_DIM)
    v = v.reshape(B, M, CROSS_HEADS, CROSS_HEAD_DIM)
    s = jnp.einsum('bshd,bmhd->bhsm', q, k).astype(jnp.float32) * (CROSS_HEAD_DIM ** -0.5)
    p = jax.nn.softmax(s, axis=-1)
    o = jnp.einsum('bhsm,bmhd->bshd', p, v.astype(jnp.float32)).reshape(B, S, CROSS_WIDTH)
    return (o.astype(h.dtype) @ w_o).astype(h.dtype)


def swiglu(h, w_gu, w_down):
    g, u = jnp.split(h @ w_gu, 2, axis=-1)
    return (jax.nn.silu(g) * u) @ w_down


def moe_swiglu(h, w_router, w_gu, w_down):
    logits = (h @ w_router).astype(jnp.float32)
    top_val, top_idx = lax.top_k(logits, TOP_K)
    top_w = jax.nn.softmax(top_val, axis=-1)
    gate = jnp.sum(jax.nn.one_hot(top_idx, N_EXPERTS, dtype=jnp.float32) * top_w[..., None], axis=-2)
    out = jnp.zeros(h.shape, jnp.float32)
    for e in range(N_EXPERTS):
        out = out + gate[..., e:e + 1] * swiglu(h, w_gu[e], w_down[e]).astype(jnp.float32)
    return out.astype(h.dtype)


def lower_bounds(lb_param):
    cs = jnp.cumsum(jax.nn.softmax(lb_param.astype(jnp.float32), axis=0), axis=0)
    return cs - cs[0]


def encoder_trunk(x, mem, mix_norm, w_in, attn_out_norm, rec_out_norm, lb_fwd, lb_bwd, w_out,
                  cross_norm, mem_norm, w_q_cross, w_kv_cross, w_o_cross, ffn_norm,
                  w_gu_dense, w_down_dense, w_router, w_gu_experts, w_down_experts, final_norm):
    cos, sin = rope_tables(x.shape[1])
    lbf = lower_bounds(lb_fwd)
    lbb = lower_bounds(lb_bwd)
    for l in range(DEPTH):
        h = rmsnorm(x, mix_norm[l])
        x = x + parallel_mixer(h, w_in[l], attn_out_norm[l], rec_out_norm[l], lbf[l], lbb[l], w_out[l], cos, sin)
        h = rmsnorm(x, cross_norm[l])
        x = x + memory_cross_attention(h, rmsnorm(mem, mem_norm[l]), w_q_cross[l], w_kv_cross[l], w_o_cross[l])
        h = rmsnorm(x, ffn_norm[l])
        if l % 2 == 0:
            x = x + swiglu(h, w_gu_dense[l // 2], w_down_dense[l // 2]).astype(x.dtype)
        else:
            x = x + moe_swiglu(h, w_router[l // 2], w_gu_experts[l // 2], w_down_experts[l // 2])
    return rmsnorm(x, final_norm)


def setup_inputs(seed: int = 0) -> dict:
    key = jax.random.key(seed)
    ks = jax.random.split(key, 25)
    nrm = lambda k, shape, fan_in: jax.random.normal(k, shape, jnp.float32) * (fan_in ** -0.5)
    gain = lambda k, shape: 1.0 + 0.02 * jax.random.normal(k, shape, jnp.float32)
    return {
        "x_prompt": jax.random.normal(ks[0], (BATCH, SEQ, D_MODEL), jnp.float32),
        "x_sample": jax.random.normal(ks[1], (DEC_BATCH, DEC_SEQ, D_MODEL), jnp.float32),
        "mem_prompt": jax.random.normal(ks[2], (BATCH, MEM_TOKENS, D_MODEL), jnp.float32),
        "mem_sample": jax.random.normal(ks[3], (DEC_BATCH, MEM_TOKENS, D_MODEL), jnp.float32),
        "mix_norm": gain(ks[4], (DEPTH, D_MODEL)),
        "w_in": nrm(ks[5], (DEPTH, D_MODEL, IN_WIDTH), D_MODEL),
        "attn_out_norm": gain(ks[6], (DEPTH, ATTN_WIDTH)),
        "rec_out_norm": gain(ks[7], (DEPTH, REC_WIDTH)),
        "lb_fwd": 0.1 * jax.random.normal(ks[8], (DEPTH, REC_WIDTH), jnp.float32),
        "lb_bwd": 0.1 * jax.random.normal(ks[9], (DEPTH, REC_WIDTH), jnp.float32),
        "w_out": nrm(ks[10], (DEPTH, MIX_WIDTH, D_MODEL), MIX_WIDTH),
        "cross_norm": gain(ks[11], (DEPTH, D_MODEL)),
        "mem_norm": gain(ks[12], (DEPTH, D_MODEL)),
        "w_q_cross": nrm(ks[13], (DEPTH, D_MODEL, CROSS_WIDTH), D_MODEL),
        "w_kv_cross": nrm(ks[14], (DEPTH, D_MODEL, 2 * CROSS_WIDTH), D_MODEL),
        "w_o_cross": nrm(ks[15], (DEPTH, CROSS_WIDTH, D_MODEL), CROSS_WIDTH),
        "ffn_norm": gain(ks[16], (DEPTH, D_MODEL)),
        "w_gu_dense": nrm(ks[17], (N_DENSE, D_MODEL, 2 * D_FF), D_MODEL),
        "w_down_dense": nrm(ks[18], (N_DENSE, D_FF, D_MODEL), D_FF),
        "w_router": nrm(ks[19], (N_MOE, D_MODEL, N_EXPERTS), D_MODEL),
        "w_gu_experts": nrm(ks[20], (N_MOE, N_EXPERTS, D_MODEL, 2 * D_FF), D_MODEL),
        "w_down_experts": nrm(ks[21], (N_MOE, N_EXPERTS, D_FF, D_MODEL), D_FF),
        "final_norm": gain(ks[22], (D_MODEL,)),
    }


def reference(x_prompt, x_sample, mem_prompt, mem_sample, mix_norm, w_in, attn_out_norm, rec_out_norm,
              lb_fwd, lb_bwd, w_out, cross_norm, mem_norm, w_q_cross, w_kv_cross, w_o_cross, ffn_norm,
              w_gu_dense, w_down_dense, w_router, w_gu_experts, w_down_experts, final_norm):
    params = (mix_norm, w_in, attn_out_norm, rec_out_norm, lb_fwd, lb_bwd, w_out, cross_norm, mem_norm,
              w_q_cross, w_kv_cross, w_o_cross, ffn_norm, w_gu_dense, w_down_dense, w_router,
              w_gu_experts, w_down_experts, final_norm)
    y_prompt = encoder_trunk(x_prompt, mem_prompt, *params)
    y_sample = encoder_trunk(x_sample, mem_sample, *params)
    return (y_prompt, y_sample)
```

```python
import functools

import jax
import jax.numpy as jnp
from jax import lax
from jax.experimental import pallas as pl
from jax.experimental.pallas import tpu as pltpu

F32 = jnp.float32
BF16 = jnp.bfloat16
I32 = jnp.int32

D_MODEL = 1024
ATTN_WIDTH = 512
REC_WIDTH = 512
ATTN_HEAD_DIM = 64
ATTN_HEADS = 8
ROT_DIM = 16
ROPE_THETA = 500000.0
DILATIONS = (1, 4, 16)
BAND_RADIUS = 64
REC_HEAD_DIM = 128
REC_HEADS = 4
REC_CHUNK = 64
REC_SUB = 16
MEM_TOKENS = 256
CROSS_HEADS = 4
CROSS_HEAD_DIM = 128
CROSS_WIDTH = 512
D_FF = 3584
N_EXPERTS = 8
IN_WIDTH = 4096
EPS = 1e-6
DEPTH = 4

LANES = 128
NEG_BIG = -1e30
MIB = 1024 * 1024

NT_DIMS = (((1,), (1,)), ((), ()))
TN_DIMS = (((0,), (0,)), ((), ()))


def _params(semantics, vmem_mib):
    return pltpu.CompilerParams(dimension_semantics=semantics, vmem_limit_bytes=vmem_mib * MIB)


def _rms_scale(x):
    return lax.rsqrt(jnp.mean(x * x, axis=-1, keepdims=True) + EPS)


def _split_bf16(a):
    hi = a.astype(BF16)
    lo = (a - hi.astype(F32)).astype(BF16)
    return hi, lo


def _norm_proj_kernel(*refs, rope, tn):
    if rope:
        x_ref, g_ref, w_ref, c_ref, s1_ref, s2_ref, o_ref, h_ref = refs
    else:
        x_ref, g_ref, w_ref, o_ref, h_ref = refs
    j = pl.program_id(1)

    @pl.when(j == 0)
    def _():
        x = x_ref[...]
        h_ref[...] = (x * _rms_scale(x) * g_ref[...]).astype(BF16)

    y = jnp.dot(h_ref[...], w_ref[...], preferred_element_type=F32)
    if not rope:
        o_ref[...] = y.astype(o_ref.dtype)
        return

    @pl.when(j < 2)
    def _():
        scale = jnp.where(j == 0, ATTN_HEAD_DIM ** -0.5, 1.0).astype(F32)
        c, s1, s2 = c_ref[...], s1_ref[...], s2_ref[...]
        for gi in range(tn // LANES):
            sl = slice(gi * LANES, (gi + 1) * LANES)
            yg = y[:, sl] * scale
            r = yg * c + pltpu.roll(yg, 8, 1) * s1 + pltpu.roll(yg, LANES - 8, 1) * s2
            o_ref[:, sl] = r.astype(o_ref.dtype)

    @pl.when(j >= 2)
    def _():
        o_ref[...] = y.astype(o_ref.dtype)


def _norm_proj(x, g, w, *, seq_len=None, rope_tabs=None, tm=512, tn=512):
    T, D = x.shape
    N = w.shape[1]
    tm = min(tm, T)
    rope = rope_tabs is not None
    in_specs = [
        pl.BlockSpec((tm, D), lambda i, j: (i, 0)),
        pl.BlockSpec((1, D), lambda i, j: (0, 0)),
        pl.BlockSpec((D, tn), lambda i, j: (0, j)),
    ]
    args = [x, g.reshape(1, D), w]
    if rope:
        tm = min(tm, seq_len)
        in_specs[0] = pl.BlockSpec((tm, D), lambda i, j: (i, 0))
        nb = seq_len // tm
        tab_spec = pl.BlockSpec((tm, LANES), lambda i, j: (i % nb, 0))
        in_specs += [tab_spec, tab_spec, tab_spec]
        args += list(rope_tabs)
    return pl.pallas_call(
        functools.partial(_norm_proj_kernel, rope=rope, tn=tn),
        grid=(T // tm, N // tn),
        in_specs=in_specs,
        out_specs=pl.BlockSpec((tm, tn), lambda i, j: (i, j)),
        out_shape=jax.ShapeDtypeStruct((T, N), BF16),
        scratch_shapes=[pltpu.VMEM((tm, D), BF16)],
        compiler_params=_params(("parallel", "arbitrary"), 32),
        name="norm_proj_rope" if rope else "norm_proj",
    )(*args)


def _rope_tables(S):
    half = ROT_DIM // 2
    inv = ROPE_THETA ** (-jnp.arange(0, ROT_DIM, 2, dtype=F32) / ROT_DIM)
    ang = jnp.arange(S, dtype=F32)[:, None] * inv[None, :]
    cos, sin = jnp.cos(ang), jnp.sin(ang)
    ones = jnp.ones((S, ATTN_HEAD_DIM - ROT_DIM), F32)
    zeros8 = jnp.zeros((S, half), F32)
    zeros48 = jnp.zeros((S, ATTN_HEAD_DIM - ROT_DIM), F32)
    c = jnp.concatenate([cos, cos, ones], axis=1)
    s1 = jnp.concatenate([zeros8, sin, zeros48], axis=1)
    s2 = jnp.concatenate([-sin, zeros8, zeros48], axis=1)
    tile2 = lambda t: jnp.concatenate([t, t], axis=1)
    return tile2(c), tile2(s1), tile2(s2)


def _band_attn_kernel(q_ref, k_ref, v_ref, o_ref, lse_ref, *, tq, w, L):
    i = pl.program_id(2)
    m0 = i * tq
    start = pl.multiple_of(jnp.clip(m0 - BAND_RADIUS, 0, L - w), min(64, w))
    kb = k_ref[pl.ds(start, w), :]
    vb = v_ref[pl.ds(start, w), :]
    qb = q_ref[...]
    qpos = m0 + lax.broadcasted_iota(I32, (tq, w), 0)
    kpos = start + lax.broadcasted_iota(I32, (tq, w), 1)
    mask = jnp.abs(qpos - kpos) <= BAND_RADIUS
    lo_half = lax.broadcasted_iota(I32, (1, LANES), 1) < ATTN_HEAD_DIM
    lane_t = lax.broadcasted_iota(I32, (tq, LANES), 1)
    lse_tile = jnp.zeros((tq, LANES), F32)
    zero = jnp.zeros((), BF16)
    one = jnp.ones((), BF16)
    for hp in range(ATTN_WIDTH // LANES):
        sl = slice(hp * LANES, (hp + 1) * LANES)
        qp, kp, vp = qb[:, sl], kb[:, sl], vb[:, sl]
        outs = []
        for sub in range(2):
            hm = lo_half if sub == 0 else jnp.logical_not(lo_half)
            qh = jnp.where(hm, qp, zero)
            s = lax.dot_general(qh, kp, NT_DIMS, preferred_element_type=F32)
            s = jnp.where(mask, s, NEG_BIG)
            m = jnp.max(s, axis=-1, keepdims=True)
            p = jnp.exp(s - m)
            va = jnp.where(hm, vp, one)
            pv = jnp.dot(p.astype(BF16), va, preferred_element_type=F32)
            lcol = pv[:, ATTN_HEAD_DIM:ATTN_HEAD_DIM + 1] if sub == 0 else pv[:, 0:1]
            outs.append(pv / lcol)
            lse_tile = jnp.where(lane_t == 2 * hp + sub, m + jnp.log(lcol), lse_tile)
        o_ref[:, sl] = jnp.where(lo_half, outs[0], outs[1]).astype(o_ref.dtype)
    lse_ref[...] = lse_tile


def _band_attn(proj, B, S, d):
    L = S // d
    tq = min(128, L)
    w = min(tq + 2 * BAND_RADIUS, L)
    slabs = IN_WIDTH // ATTN_WIDTH
    view = proj.reshape(B, L, d * IN_WIDTH)
    o, lse = pl.pallas_call(
        functools.partial(_band_attn_kernel, tq=tq, w=w, L=L),
        grid=(B, d, L // tq),
        in_specs=[
            pl.BlockSpec((None, tq, ATTN_WIDTH), lambda b, r, i: (b, i, r * slabs)),
            pl.BlockSpec((None, L, ATTN_WIDTH), lambda b, r, i: (b, 0, r * slabs + 1)),
            pl.BlockSpec((None, L, ATTN_WIDTH), lambda b, r, i: (b, 0, r * slabs + 2)),
        ],
        out_specs=[
            pl.BlockSpec((None, tq, ATTN_WIDTH), lambda b, r, i: (b, i, r)),
            pl.BlockSpec((None, tq, LANES), lambda b, r, i: (b, i, r)),
        ],
        out_shape=[
            jax.ShapeDtypeStruct((B, L, d * ATTN_WIDTH), BF16),
            jax.ShapeDtypeStruct((B, L, d * LANES), F32),
        ],
        compiler_params=_params(("parallel", "parallel", "arbitrary"), 40),
        name=f"band_attn_d{d}",
    )(view, view, view)
    return o.reshape(B * S, ATTN_WIDTH), lse.reshape(B * S, LANES)


def _hgrn_chunk(c0, q_ref, v_ref, z_ref, lb, tri, st_ref, out_ref, rev, row16):
    C, U = REC_CHUNK, REC_SUB
    rows = pl.ds(c0, C)
    q = q_ref[rows, :].astype(F32)
    vb = v_ref[rows, :]
    vf = vb.astype(F32)
    z = z_ref[rows, :].astype(F32)
    e = jnp.exp(-jnp.abs(z))
    r = 1.0 / (1.0 + e)
    er = e * r
    pos = z >= 0
    sig_p = jnp.where(pos, r, er)
    sig_n = jnp.where(pos, er, r)
    g = jnp.log(lb + (1.0 - lb) * sig_p)
    kk = (1.0 - lb) * sig_n
    g_hi, g_lo = _split_bf16(g)
    b = jnp.dot(tri, g_hi, preferred_element_type=F32) + jnp.dot(tri, g_lo, preferred_element_type=F32)
    bend = b[0:1, :] if rev else b[C - 1:C, :]
    qt = (q * jnp.exp(b)).astype(BF16)
    kbar = (kk * jnp.exp(bend - b)).astype(BF16)
    st = st_ref[...]
    o = lax.dot_general(qt, st.astype(BF16), NT_DIMS, preferred_element_type=F32)
    ut = lax.dot_general(vb, kbar, TN_DIMS, preferred_element_type=F32)
    st_ref[...] = st * jnp.exp(bend) + ut
    parts = []
    for i in range(C // U):
        ri = slice(i * U, (i + 1) * U)
        bq, qq, kq, vq = b[ri], q[ri], kk[ri], vf[ri]
        acc = o[ri]
        if not rev and i > 0:
            ei, beta = slice(0, i * U), b[i * U - 1:i * U]
        elif rev and i < C // U - 1:
            ei, beta = slice((i + 1) * U, C), b[(i + 1) * U:(i + 1) * U + 1]
        else:
            ei = None
        if ei is not None:
            qi = (qq * jnp.exp(bq - beta)).astype(BF16)
            ki = (kk[ei] * jnp.exp(beta - b[ei])).astype(BF16)
            pad = jnp.zeros((C - ki.shape[0], REC_HEAD_DIM), BF16)
            ki = jnp.concatenate([pad, ki], axis=0) if rev else jnp.concatenate([ki, pad], axis=0)
            sc = lax.dot_general(qi, ki, NT_DIMS, preferred_element_type=F32)
            acc = acc + jnp.dot(sc.astype(BF16), vb, preferred_element_type=F32)
        for s in range(U):
            valid = (row16 <= s) if rev else (row16 >= s)
            dlt = jnp.where(valid, bq - bq[s:s + 1], 0.0)
            ex = jnp.where(valid, jnp.exp(dlt), 0.0)
            col = jnp.sum(qq * ex * kq[s:s + 1], axis=-1, keepdims=True)
            acc = acc + col * vq[s:s + 1]
        parts.append(acc)
    out_ref[rows, :] = jnp.concatenate(parts, axis=0)


def _hgrn_kernel(q_ref, i_ref, zf_ref, zb_ref, gr_ref, lbf_ref, lbb_ref, rg_ref, o_ref,
                 of_ref, ob_ref, stf_ref, stb_ref, *, S):
    C = REC_CHUNK
    nc = S // C
    row = lax.broadcasted_iota(I32, (C, C), 0)
    col = lax.broadcasted_iota(I32, (C, C), 1)
    tri_f = (col <= row).astype(BF16)
    tri_b = (col >= row).astype(BF16)
    row16 = lax.broadcasted_iota(I32, (REC_SUB, 1), 0)
    stf_ref[...] = jnp.zeros_like(stf_ref)
    stb_ref[...] = jnp.zeros_like(stb_ref)
    lbf = lbf_ref[...]
    lbb = lbb_ref[...]

    def body(c, carry):
        cf = pl.multiple_of(c * C, C)
        cb = pl.multiple_of((nc - 1 - c) * C, C)
        _hgrn_chunk(cf, q_ref, i_ref, zf_ref, lbf, tri_f, stf_ref, of_ref, False, row16)
        _hgrn_chunk(cb, q_ref, i_ref, zb_ref, lbb, tri_b, stb_ref, ob_ref, True, row16)
        return carry

    lax.fori_loop(0, nc, body, 0)

    tb = min(256, S)
    rg = rg_ref[...]

    def fin(t, carry):
        rows = pl.ds(pl.multiple_of(t * tb, tb), tb)
        o = of_ref[rows, :] + ob_ref[rows, :]
        gate = gr_ref[rows, :].astype(F32)
        silu = gate / (1.0 + jnp.exp(-gate))
        o_ref[rows, :] = (o * _rms_scale(o) * rg * silu).astype(o_ref.dtype)
        return carry

    lax.fori_loop(0, S // tb, fin, 0)


def _hgrn(proj, B, S, lb_f, lb_b, rec_g):
    view = proj.reshape(B, S, IN_WIDTH)
    per = REC_WIDTH // LANES
    col = lambda slab: (lambda b, h: (b, 0, slab * per + h))
    seq_spec = lambda slab: pl.BlockSpec((None, S, LANES), col(slab))
    vec_spec = pl.BlockSpec((1, LANES), lambda b, h: (0, h))
    out = pl.pallas_call(
        functools.partial(_hgrn_kernel, S=S),
        grid=(B, REC_HEADS),
        in_specs=[seq_spec(3), seq_spec(4), seq_spec(5), seq_spec(6), seq_spec(7),
                  vec_spec, vec_spec, vec_spec],
        out_specs=pl.BlockSpec((None, S, LANES), lambda b, h: (b, 0, h)),
        out_shape=jax.ShapeDtypeStruct((B, S, REC_WIDTH), BF16),
        scratch_shapes=[pltpu.VMEM((S, LANES), F32), pltpu.VMEM((S, LANES), F32),
                        pltpu.VMEM((REC_HEAD_DIM, REC_HEAD_DIM), F32),
                        pltpu.VMEM((REC_HEAD_DIM, REC_HEAD_DIM), F32)],
        compiler_params=_params(("parallel", "parallel"), 40),
        name="hgrn2_bidir",
    )(view, view, view, view, view, lb_f.reshape(1, REC_WIDTH), lb_b.reshape(1, REC_WIDTH),
      rec_g.reshape(1, REC_WIDTH))
    return out.reshape(B * S, REC_WIDTH)


def _mix_out_kernel(o1_ref, o2_ref, o3_ref, l1_ref, l2_ref, l3_ref, rec_ref, ag_ref, w_ref, x_ref, out_ref):
    l1, l2, l3 = l1_ref[...], l2_ref[...], l3_ref[...]
    m = jnp.maximum(jnp.maximum(l1, l2), l3)
    e1, e2, e3 = jnp.exp(l1 - m), jnp.exp(l2 - m), jnp.exp(l3 - m)
    inv = 1.0 / (e1 + e2 + e3)
    hrow = lax.broadcasted_iota(I32, (LANES, ATTN_WIDTH), 0)
    hcol = lax.broadcasted_iota(I32, (LANES, ATTN_WIDTH), 1) // ATTN_HEAD_DIM
    expand = (hrow == hcol).astype(BF16)

    def widen(wgt):
        hi, lo = _split_bf16(wgt)
        return (jnp.dot(hi, expand, preferred_element_type=F32)
                + jnp.dot(lo, expand, preferred_element_type=F32))

    att = (widen(e1 * inv) * o1_ref[...].astype(F32) + widen(e2 * inv) * o2_ref[...].astype(F32)
           + widen(e3 * inv) * o3_ref[...].astype(F32))
    prow = lax.broadcasted_iota(I32, (ATTN_WIDTH, ATTN_WIDTH), 0) // ATTN_HEAD_DIM
    pcol = lax.broadcasted_iota(I32, (ATTN_WIDTH, ATTN_WIDTH), 1) // ATTN_HEAD_DIM
    pool = jnp.where(prow == pcol, 1.0 / ATTN_HEAD_DIM, 0.0).astype(BF16)
    sq_hi, sq_lo = _split_bf16(att * att)
    ms = jnp.dot(sq_hi, pool, preferred_element_type=F32) + jnp.dot(sq_lo, pool, preferred_element_type=F32)
    attn = (att * lax.rsqrt(ms + EPS) * ag_ref[...]).astype(BF16)
    y = jnp.dot(attn, w_ref[0:ATTN_WIDTH, :], preferred_element_type=F32)
    y = y + jnp.dot(rec_ref[...], w_ref[ATTN_WIDTH:, :], preferred_element_type=F32)
    out_ref[...] = x_ref[...] + y


def _mix_out(x, outs, lses, rec, attn_g, w_out, tm=512):
    T = x.shape[0]
    tm = min(tm, T)
    row = lambda n: pl.BlockSpec((tm, n), lambda i: (i, 0))
    return pl.pallas_call(
        _mix_out_kernel,
        grid=(T // tm,),
        in_specs=[row(ATTN_WIDTH)] * 3 + [row(LANES)] * 3 + [
            row(REC_WIDTH),
            pl.BlockSpec((1, ATTN_WIDTH), lambda i: (0, 0)),
            pl.BlockSpec((D_MODEL, D_MODEL), lambda i: (0, 0)),
            row(D_MODEL)],
        out_specs=row(D_MODEL),
        out_shape=jax.ShapeDtypeStruct((T, D_MODEL), F32),
        compiler_params=_params(("parallel",), 40),
        name="mix_out",
    )(*outs, *lses, rec, attn_g.reshape(1, ATTN_WIDTH), w_out, x)


def _cross_kernel(x_ref, g_ref, wq_ref, kv_ref, wo_ref, out_ref):
    x = x_ref[...]
    h = (x * _rms_scale(x) * g_ref[...]).astype(BF16)
    q = jnp.dot(h, wq_ref[...], preferred_element_type=F32).astype(BF16)
    heads = []
    for hd in range(CROSS_HEADS):
        sl = slice(hd * CROSS_HEAD_DIM, (hd + 1) * CROSS_HEAD_DIM)
        kh = kv_ref[:, sl]
        vh = kv_ref[:, CROSS_WIDTH + hd * CROSS_HEAD_DIM:CROSS_WIDTH + (hd + 1) * CROSS_HEAD_DIM]
        s = lax.dot_general(q[:, sl], kh, NT_DIMS, preferred_element_type=F32) * (CROSS_HEAD_DIM ** -0.5)
        m = jnp.max(s, axis=-1, keepdims=True)
        p = jnp.exp(s - m)
        l = jnp.sum(p, axis=-1, keepdims=True)
        o = jnp.dot(p.astype(BF16), vh, preferred_element_type=F32) / l
        heads.append(o.astype(BF16))
    o = jnp.concatenate(heads, axis=-1)
    out_ref[...] = x + jnp.dot(o, wo_ref[...], preferred_element_type=F32)


def _cross_attn(x, B, S, g, w_q, kv, w_o, tm=256):
    tm = min(tm, S)
    nb = S // tm
    return pl.pallas_call(
        _cross_kernel,
        grid=(B, nb),
        in_specs=[
            pl.BlockSpec((tm, D_MODEL), lambda b, i: (b * nb + i, 0)),
            pl.BlockSpec((1, D_MODEL), lambda b, i: (0, 0)),
            pl.BlockSpec((D_MODEL, CROSS_WIDTH), lambda b, i: (0, 0)),
            pl.BlockSpec((None, MEM_TOKENS, 2 * CROSS_WIDTH), lambda b, i: (b, 0, 0)),
            pl.BlockSpec((CROSS_WIDTH, D_MODEL), lambda b, i: (0, 0)),
        ],
        out_specs=pl.BlockSpec((tm, D_MODEL), lambda b, i: (b * nb + i, 0)),
        out_shape=jax.ShapeDtypeStruct(x.shape, F32),
        compiler_params=_params(("parallel", "parallel"), 40),
        name="cross_attn",
    )(x, g.reshape(1, D_MODEL), w_q, kv.reshape(B, MEM_TOKENS, 2 * CROSS_WIDTH), w_o)


def _swiglu_tile(h, wg, wu, wd):
    g = jnp.dot(h, wg, preferred_element_type=F32)
    u = jnp.dot(h, wu, preferred_element_type=F32)
    act = (g / (1.0 + jnp.exp(-g)) * u).astype(BF16)
    return jnp.dot(act, wd, preferred_element_type=F32)


def _ffn_kernel(x_ref, g_ref, wg_ref, wu_ref, wd_ref, out_ref, h_ref, acc_ref):
    f = pl.program_id(1)

    @pl.when(f == 0)
    def _():
        x = x_ref[...]
        h_ref[...] = (x * _rms_scale(x) * g_ref[...]).astype(BF16)
        acc_ref[...] = x

    acc_ref[...] += _swiglu_tile(h_ref[...], wg_ref[...], wu_ref[...], wd_ref[...])

    @pl.when(f == pl.num_programs(1) - 1)
    def _():
        out_ref[...] = acc_ref[...]


def _ffn_dense(x, g, w_gu, w_down, tm=1024, tf=512):
    T = x.shape[0]
    tm = min(tm, T)
    nf = D_FF // tf
    return pl.pallas_call(
        _ffn_kernel,
        grid=(T // tm, nf),
        in_specs=[
            pl.BlockSpec((tm, D_MODEL), lambda i, f: (i, 0)),
            pl.BlockSpec((1, D_MODEL), lambda i, f: (0, 0)),
            pl.BlockSpec((D_MODEL, tf), lambda i, f: (0, f)),
            pl.BlockSpec((D_MODEL, tf), lambda i, f: (0, nf + f)),
            pl.BlockSpec((tf, D_MODEL), lambda i, f: (f, 0)),
        ],
        out_specs=pl.BlockSpec((tm, D_MODEL), lambda i, f: (i, 0)),
        out_shape=jax.ShapeDtypeStruct(x.shape, F32),
        scratch_shapes=[pltpu.VMEM((tm, D_MODEL), BF16), pltpu.VMEM((tm, D_MODEL), F32)],
        compiler_params=_params(("parallel", "arbitrary"), 48),
        name="ffn_dense",
    )(x, g.reshape(1, D_MODEL), w_gu, w_gu, w_down)


def _router_kernel(x_ref, g_ref, wr_ref, sel_ref, gate_ref):
    x = x_ref[...]
    h = x * _rms_scale(x) * g_ref[...]
    h_hi, h_lo = _split_bf16(h)
    w = wr_ref[...]
    w_hi, w_lo = _split_bf16(w)
    logits = (jnp.dot(h_hi, w_hi, preferred_element_type=F32) + jnp.dot(h_hi, w_lo, preferred_element_type=F32)
              + jnp.dot(h_lo, w_hi, preferred_element_type=F32))
    lane = lax.broadcasted_iota(I32, logits.shape, 1)
    logits = jnp.where(lane < N_EXPERTS, logits, NEG_BIG)
    v1 = jnp.max(logits, axis=-1, keepdims=True)
    i1 = jnp.min(jnp.where(logits == v1, lane, LANES), axis=-1, keepdims=True)
    rest = jnp.where(lane == i1, NEG_BIG, logits)
    v2 = jnp.max(rest, axis=-1, keepdims=True)
    i2 = jnp.min(jnp.where(rest == v2, lane, LANES), axis=-1, keepdims=True)
    e2 = jnp.exp(v2 - v1)
    w1 = 1.0 / (1.0 + e2)
    w2 = e2 * w1
    sel_ref[...] = jnp.where(lane == 0, i1, jnp.where(lane == 1, i2, 0))
    gate_ref[...] = jnp.where(lane == 0, w1, jnp.where(lane == 1, w2, 0.0))


def _router(x, g, w_router_pad, tm=512):
    T = x.shape[0]
    tm = min(tm, T)
    return pl.pallas_call(
        _router_kernel,
        grid=(T // tm,),
        in_specs=[
            pl.BlockSpec((tm, D_MODEL), lambda i: (i, 0)),
            pl.BlockSpec((1, D_MODEL), lambda i: (0, 0)),
            pl.BlockSpec((D_MODEL, LANES), lambda i: (0, 0)),
        ],
        out_specs=[pl.BlockSpec((tm, LANES), lambda i: (i, 0)), pl.BlockSpec((tm, LANES), lambda i: (i, 0))],
        out_shape=[jax.ShapeDtypeStruct((T, LANES), I32), jax.ShapeDtypeStruct((T, LANES), F32)],
        compiler_params=_params(("parallel",), 32),
        name="moe_router",
    )(x, g.reshape(1, D_MODEL), w_router_pad)


def _row_copy(src_hbm, dst_vmem, src_row, dst_row, sem):
    return pltpu.make_async_copy(src_hbm.at[pl.ds(src_row, 1), :], dst_vmem.at[pl.ds(dst_row, 1), :], sem)


def _moe_ffn_kernel(te_ref, nu_ref, tok_ref, x_hbm, g_ref, wg_ref, wu_ref, wd_ref, out_ref,
                    xg_ref, h_ref, acc_ref, sem, *, tm):
    j = pl.program_id(0)
    f = pl.program_id(1)
    used = j < nu_ref[0]

    @pl.when(jnp.logical_and(used, f == 0))
    def _():
        def start(r, c):
            _row_copy(x_hbm, xg_ref, tok_ref[0, 0, r], r, sem).start()
            return c

        lax.fori_loop(0, tm, start, 0)

        def wait(r, c):
            _row_copy(x_hbm, xg_ref, 0, r, sem).wait()
            return c

        lax.fori_loop(0, tm, wait, 0)
        x = xg_ref[...]
        h_ref[...] = (x * _rms_scale(x) * g_ref[...]).astype(BF16)
        acc_ref[...] = jnp.zeros_like(acc_ref)

    @pl.when(used)
    def _():
        acc_ref[...] += _swiglu_tile(h_ref[...], wg_ref[...], wu_ref[...], wd_ref[...])

    @pl.when(f == pl.num_programs(1) - 1)
    def _():
        out_ref[...] = acc_ref[...]


def _moe_ffn(x, g, w_gu_e, w_down_e, tile_expert, n_used, tok_sorted, tm, tf=512):
    n_tiles = tile_expert.shape[0]
    nf = D_FF // tf
    grid_spec = pltpu.PrefetchScalarGridSpec(
        num_scalar_prefetch=2,
        grid=(n_tiles, nf),
        in_specs=[
            pl.BlockSpec((1, 1, tm), lambda j, f, te, nu: (j, 0, 0), memory_space=pltpu.SMEM),
            pl.BlockSpec(memory_space=pl.ANY),
            pl.BlockSpec((1, D_MODEL), lambda j, f, te, nu: (0, 0)),
            pl.BlockSpec((None, D_MODEL, tf), lambda j, f, te, nu: (te[j], 0, f)),
            pl.BlockSpec((None, D_MODEL, tf), lambda j, f, te, nu: (te[j], 0, nf + f)),
            pl.BlockSpec((None, tf, D_MODEL), lambda j, f, te, nu: (te[j], f, 0)),
        ],
        out_specs=pl.BlockSpec((tm, D_MODEL), lambda j, f, te, nu: (j, 0)),
        scratch_shapes=[pltpu.VMEM((tm, D_MODEL), F32), pltpu.VMEM((tm, D_MODEL), BF16),
                        pltpu.VMEM((tm, D_MODEL), F32), pltpu.SemaphoreType.DMA(())],
    )
    return pl.pallas_call(
        functools.partial(_moe_ffn_kernel, tm=tm),
        grid_spec=grid_spec,
        out_shape=jax.ShapeDtypeStruct((n_tiles * tm, D_MODEL), F32),
        compiler_params=_params(("arbitrary", "arbitrary"), 48),
        name="moe_grouped_ffn",
    )(tile_expert, n_used, tok_sorted.reshape(n_tiles, 1, tm), x, g.reshape(1, D_MODEL), w_gu_e, w_gu_e, w_down_e)


def _moe_combine_kernel(pos_ref, x_ref, gate_ref, y_hbm, fg_ref, out_ref, y1_ref, y2_ref, sem, *, tm, final):
    def start(r, c):
        _row_copy(y_hbm, y1_ref, pos_ref[0, 0, r], r, sem).start()
        _row_copy(y_hbm, y2_ref, pos_ref[0, 1, r], r, sem).start()
        return c

    lax.fori_loop(0, tm, start, 0)

    def wait(r, c):
        _row_copy(y_hbm, y1_ref, 0, r, sem).wait()
        _row_copy(y_hbm, y2_ref, 0, r, sem).wait()
        return c

    lax.fori_loop(0, tm, wait, 0)
    gate = gate_ref[...]
    out = x_ref[...] + gate[:, 0:1] * y1_ref[...] + gate[:, 1:2] * y2_ref[...]
    if final:
        out = out * _rms_scale(out) * fg_ref[...]
    out_ref[...] = out


def _moe_combine(x, gate, ys, pos, final_g, final, tm=256):
    T = x.shape[0]
    tm = min(tm, T)
    nt = T // tm
    pos_t = pos.reshape(nt, tm, 2).transpose(0, 2, 1)
    return pl.pallas_call(
        functools.partial(_moe_combine_kernel, tm=tm, final=final),
        grid=(nt,),
        in_specs=[
            pl.BlockSpec((1, 2, tm), lambda i: (i, 0, 0), memory_space=pltpu.SMEM),
            pl.BlockSpec((tm, D_MODEL), lambda i: (i, 0)),
            pl.BlockSpec((tm, LANES), lambda i: (i, 0)),
            pl.BlockSpec(memory_space=pl.ANY),
            pl.BlockSpec((1, D_MODEL), lambda i: (0, 0)),
        ],
        out_specs=pl.BlockSpec((tm, D_MODEL), lambda i: (i, 0)),
        out_shape=jax.ShapeDtypeStruct(x.shape, F32),
        scratch_shapes=[pltpu.VMEM((tm, D_MODEL), F32), pltpu.VMEM((tm, D_MODEL), F32),
                        pltpu.SemaphoreType.DMA(())],
        compiler_params=_params(("arbitrary",), 32),
        name="moe_combine",
    )(pos_t, x, gate, ys, final_g.reshape(1, D_MODEL))


def _moe_layer(x, g, w_router_pad, w_gu_e, w_down_e, final_g, final, tm=512):
    T = x.shape[0]
    tm = min(tm, T)
    sel, gate = _router(x, g, w_router_pad)
    experts = sel[:, :2].reshape(-1)
    onehot = (experts[:, None] == jnp.arange(N_EXPERTS, dtype=I32)[None, :]).astype(I32)
    rank = jnp.sum((jnp.cumsum(onehot, axis=0) - onehot) * onehot, axis=1)
    counts = jnp.sum(onehot, axis=0)
    tiles_per = (counts + tm - 1) // tm
    tile_end = jnp.cumsum(tiles_per)
    group_start = (tile_end - tiles_per) * tm
    pos = group_start[experts] + rank
    n_tiles = (2 * T) // tm + N_EXPERTS
    tok = jnp.arange(2 * T, dtype=I32) // 2
    tok_sorted = jnp.zeros((n_tiles * tm,), I32).at[pos].set(tok)
    tile_expert = jnp.minimum(jnp.searchsorted(tile_end, jnp.arange(n_tiles, dtype=I32), side="right"),
                              N_EXPERTS - 1).astype(I32)
    n_used = tile_end[-1:].astype(I32)
    ys = _moe_ffn(x, g, w_gu_e, w_down_e, tile_expert, n_used, tok_sorted, tm)
    return _moe_combine(x, gate, ys, pos.reshape(T, 2).astype(I32), final_g, final)


def _lower_bounds(lb_param):
    cs = jnp.cumsum(jax.nn.softmax(lb_param.astype(F32), axis=0), axis=0)
    return cs - cs[0]


def _trunk(x, mem, p):
    B, S, _ = x.shape
    x = x.reshape(B * S, D_MODEL)
    mem = mem.reshape(B * MEM_TOKENS, D_MODEL)
    tabs = _rope_tables(S)
    for l in range(DEPTH):
        proj = _norm_proj(x, p["mix_norm"][l], p["w_in"][l], seq_len=S, rope_tabs=tabs)
        branches = [_band_attn(proj, B, S, d) for d in DILATIONS]
        rec = _hgrn(proj, B, S, p["lbf"][l], p["lbb"][l], p["rec_out_norm"][l])
        x = _mix_out(x, [o for o, _ in branches], [s for _, s in branches], rec,
                     p["attn_out_norm"][l], p["w_out"][l])
        kv = _norm_proj(mem, p["mem_norm"][l], p["w_kv_cross"][l])
        x = _cross_attn(x, B, S, p["cross_norm"][l], p["w_q_cross"][l], kv, p["w_o_cross"][l])
        if l % 2 == 0:
            x = _ffn_dense(x, p["ffn_norm"][l], p["w_gu_dense"][l // 2], p["w_down_dense"][l // 2])
        else:
            x = _moe_layer(x, p["ffn_norm"][l], p["w_router"][l // 2], p["w_gu_experts"][l // 2],
                           p["w_down_experts"][l // 2], p["final_norm"], final=(l == DEPTH - 1))
    return x.reshape(B, S, D_MODEL)


def kernel(x_prompt, x_sample, mem_prompt, mem_sample, mix_norm, w_in, attn_out_norm, rec_out_norm, lb_fwd, lb_bwd, w_out, cross_norm, mem_norm, w_q_cross, w_kv_cross, w_o_cross, ffn_norm, w_gu_dense, w_down_dense, w_router, w_gu_experts, w_down_experts, final_norm):
    bf = lambda w: w.astype(BF16)
    p = dict(
        mix_norm=mix_norm, w_in=bf(w_in), attn_out_norm=attn_out_norm, rec_out_norm=rec_out_norm,
        lbf=_lower_bounds(lb_fwd), lbb=_lower_bounds(lb_bwd), w_out=bf(w_out),
        cross_norm=cross_norm, mem_norm=mem_norm, w_q_cross=bf(w_q_cross), w_kv_cross=bf(w_kv_cross),
        w_o_cross=bf(w_o_cross), ffn_norm=ffn_norm, w_gu_dense=bf(w_gu_dense), w_down_dense=bf(w_down_dense),
        w_router=jnp.pad(w_router, ((0, 0), (0, 0), (0, LANES - N_EXPERTS))),
        w_gu_experts=bf(w_gu_experts), w_down_experts=bf(w_down_experts), final_norm=final_norm,
    )
    return (_trunk(x_prompt, mem_prompt, p), _trunk(x_sample, mem_sample, p))
```
